```python
import jax
import jax.numpy as jnp
from jax import lax
import numpy as np

D_MODEL = 1024
BATCH = 8
SEQ = 8192
DEPTH = 1

D_LRU = 1024
LRU_BLOCKS = 4
LRU_BLOCK_W = D_LRU // LRU_BLOCKS
CONV_W = 4
LRU_C = 8.0
D_HGRN = 1024
HGRN_EXPAND = 128
HGRN_HEADS = D_HGRN // HGRN_EXPAND
HGRN_DK = HGRN_EXPAND
HGRN_DV = D_HGRN // HGRN_HEADS
CHUNK = 64
D_MIX = D_LRU + D_HGRN
D_IN = 2 * D_LRU + 4 * D_HGRN
EPS = 1e-6

kernel_name = 'hymba_style_rglru_hgrn2_block'


def rmsnorm(x, w):
    xf = x.astype(jnp.float32)
    y = xf * lax.rsqrt(jnp.mean(xf * xf, axis=-1, keepdims=True) + EPS)
    return (y * w.astype(jnp.float32)).astype(x.dtype)


def causal_depthwise_conv(x, w, b):
    seq = x.shape[1]
    xp = jnp.pad(x, ((0, 0), (CONV_W - 1, 0), (0, 0)))
    y = b
    for k in range(CONV_W):
        y = y + w[k] * xp[:, k:k + seq, :]
    return y


def rg_lru(x, w_a, b_a, w_x, b_x, lam):
    bsz, seq, _ = x.shape
    xf = x.astype(jnp.float32)
    xb = xf.reshape(bsz, seq, LRU_BLOCKS, LRU_BLOCK_W)
    r = jax.nn.sigmoid(jnp.einsum('bsnh,nhk->bsnk', xb, w_a.astype(jnp.float32)) + b_a.astype(jnp.float32)).reshape(bsz, seq, D_LRU)
    i = jax.nn.sigmoid(jnp.einsum('bsnh,nhk->bsnk', xb, w_x.astype(jnp.float32)) + b_x.astype(jnp.float32)).reshape(bsz, seq, D_LRU)
    log_a = -LRU_C * r * jax.nn.softplus(-lam.astype(jnp.float32))
    a = jnp.exp(log_a)
    u = jnp.sqrt(-jnp.expm1(2.0 * log_a)) * (i * xf)

    def combine(left, right):
        a_l, h_l = left
        a_r, h_r = right
        return a_l * a_r, a_r * h_l + h_r

    _, h = lax.associative_scan(combine, (a, u), axis=1)
    return h.astype(x.dtype)


def hgrn2_chunked(q, k, v, g):
    bsz, seq = q.shape[0], q.shape[1]
    n_chunks = seq // CHUNK

    def to_chunks(t):
        return t.reshape(bsz, n_chunks, CHUNK, HGRN_HEADS, t.shape[-1]).transpose(1, 0, 3, 2, 4)

    mask = jnp.tril(jnp.ones((CHUNK, CHUNK), dtype=bool))

    def step(state, inp):
        qc, kc, vc, gc = inp
        b = jnp.cumsum(gc, axis=2)
        o_inter = jnp.einsum('bhtk,bhkv->bhtv', qc * jnp.exp(b), state)
        diff = b[:, :, :, None, :] - b[:, :, None, :, :]
        decay = jnp.where(mask[:, :, None], jnp.exp(jnp.minimum(diff, 0.0)), 0.0)
        scores = jnp.einsum('bhtk,bhsk,bhtsk->bhts', qc, kc, decay)
        o_intra = jnp.einsum('bhts,bhsv->bhtv', scores, vc)
        b_last = b[:, :, -1:, :]
        new_state = jnp.exp(b_last[:, :, 0, :, None]) * state + jnp.einsum('bhsk,bhsv->bhkv', kc * jnp.exp(b_last - b), vc)
        return new_state, o_inter + o_intra

    s0 = jnp.zeros((bsz, HGRN_HEADS, HGRN_DK, HGRN_DV), jnp.float32)
    _, o = lax.scan(step, s0, (to_chunks(q), to_chunks(k), to_chunks(v), to_chunks(g)))
    return o.transpose(1, 0, 3, 2, 4).reshape(bsz, seq, HGRN_HEADS, HGRN_DV)


def setup_inputs(seed: int = 0) -> dict:
    key = jax.random.key(seed)
    ks = jax.random.split(key, 14)
    f32 = jnp.float32

    def nrm(k, shape, scale):
        return scale * jax.random.normal(k, shape, f32)

    x = nrm(ks[0], (BATCH, SEQ, D_MODEL), 1.0)
    pre_norm_w = 1.0 + nrm(ks[1], (DEPTH, D_MODEL), 0.05)
    w_in = nrm(ks[2], (DEPTH, D_MODEL, D_IN), D_MODEL ** -0.5)
    conv_w = nrm(ks[3], (DEPTH, CONV_W, D_LRU), CONV_W ** -0.5)
    conv_b = nrm(ks[4], (DEPTH, D_LRU), 0.01)
    lru_w_a = nrm(ks[5], (DEPTH, LRU_BLOCKS, LRU_BLOCK_W, LRU_BLOCK_W), LRU_BLOCK_W ** -0.5)
    lru_b_a = nrm(ks[6], (DEPTH, LRU_BLOCKS, LRU_BLOCK_W), 0.01)
    lru_w_x = nrm(ks[7], (DEPTH, LRU_BLOCKS, LRU_BLOCK_W, LRU_BLOCK_W), LRU_BLOCK_W ** -0.5)
    lru_b_x = nrm(ks[8], (DEPTH, LRU_BLOCKS, LRU_BLOCK_W), 0.01)
    a_pow_c = jax.random.uniform(ks[9], (DEPTH, D_LRU), f32, 0.9, 0.999)
    s = a_pow_c ** (1.0 / LRU_C)
    lru_lambda = jnp.log(s) - jnp.log1p(-s)
    hgrn_lb_logits = nrm(ks[10], (DEPTH + 1, D_HGRN), 0.1)
    hgrn_gnorm_w = 1.0 + nrm(ks[11], (DEPTH, D_HGRN), 0.05)
    w_out = nrm(ks[12], (DEPTH, D_MIX, D_MODEL), D_MIX ** -0.5)
    post_norm_w = 1.0 + nrm(ks[13], (DEPTH, D_MODEL), 0.05)
    return {'x': x, 'pre_norm_w': pre_norm_w, 'w_in': w_in, 'conv_w': conv_w, 'conv_b': conv_b,
            'lru_w_a': lru_w_a, 'lru_b_a': lru_b_a, 'lru_w_x': lru_w_x, 'lru_b_x': lru_b_x,
            'lru_lambda': lru_lambda, 'hgrn_lb_logits': hgrn_lb_logits, 'hgrn_gnorm_w': hgrn_gnorm_w,
            'w_out': w_out, 'post_norm_w': post_norm_w}


def reference(x, pre_norm_w, w_in, conv_w, conv_b, lru_w_a, lru_b_a, lru_w_x, lru_b_x,
              lru_lambda, hgrn_lb_logits, hgrn_gnorm_w, w_out, post_norm_w):
    bsz, seq, _ = x.shape
    dt = x.dtype
    f32 = jnp.float32
    lower_bounds = jnp.cumsum(jax.nn.softmax(hgrn_lb_logits.astype(f32), axis=0), axis=0)
    split_at = [D_LRU, 2 * D_LRU, 2 * D_LRU + D_HGRN, 2 * D_LRU + 2 * D_HGRN, 2 * D_LRU + 3 * D_HGRN]

    def heads(t):
        return t.reshape(bsz, seq, HGRN_HEADS, -1)

    h = x
    for layer in range(DEPTH):
        u = rmsnorm(h, pre_norm_w[layer])
        p = jnp.einsum('bsd,de->bse', u, w_in[layer])
        lru_x, lru_gate, q, f_raw, v, hgrn_gate = jnp.split(p, split_at, axis=-1)

        xc = causal_depthwise_conv(lru_x, conv_w[layer], conv_b[layer])
        y_lru = rg_lru(xc, lru_w_a[layer], lru_b_a[layer], lru_w_x[layer], lru_b_x[layer],
                       lru_lambda[layer]) * jax.nn.silu(lru_gate)

        lb = lower_bounds[layer]
        f = lb + (1.0 - lb) * jax.nn.sigmoid(f_raw.astype(f32))
        o = hgrn2_chunked(heads(jax.nn.silu(q.astype(f32))), heads(1.0 - f),
                          heads(v.astype(f32)), heads(jnp.log(f)))
        o = rmsnorm(o, hgrn_gnorm_w[layer].reshape(HGRN_HEADS, HGRN_DV)).reshape(bsz, seq, D_HGRN)
        y_hgrn = (o * jax.nn.silu(hgrn_gate.astype(f32))).astype(dt)

        y = jnp.einsum('bsm,md->bsd', jnp.concatenate([y_lru.astype(dt), y_hgrn], axis=-1), w_out[layer])
        h = h + rmsnorm(y, post_norm_w[layer]).astype(dt)
    return h
```

```python
import functools

import jax
import jax.numpy as jnp
from jax import lax
from jax.experimental import pallas as pl
from jax.experimental.pallas import tpu as pltpu

D_MODEL = 1024
D_LRU = 1024
LRU_BLOCKS = 4
LRU_BLOCK_W = D_LRU // LRU_BLOCKS
CONV_W = 4
LRU_C = 8.0
D_HGRN = 1024
HEADS = 8
HEAD_W = D_HGRN // HEADS
EPS = 1e-6

SUBLANES = 8
TILE = 256
CHUNK = 64
SLAB = 2 * SUBLANES
NORM_ROWS = 32
VMEM_LIMIT_BYTES = 56 * 1024 * 1024

f32 = jnp.float32
bf16 = jnp.bfloat16


def _sigmoid(x):
    return 1.0 / (1.0 + jnp.exp(-x))


def _silu(x):
    return x * _sigmoid(x)


def _scan8(a, u):
    row = lax.broadcasted_iota(jnp.int32, a.shape, 0)
    for s in (1, 2, 4):
        keep = row >= s
        a_prev = jnp.where(keep, pltpu.roll(a, s, 0), 1.0)
        u_prev = jnp.where(keep, pltpu.roll(u, s, 0), 0.0)
        u = a * u_prev + u
        a = a * a_prev
    return a, u


def _cumprod_rows(x):
    row = lax.broadcasted_iota(jnp.int32, x.shape, 0)
    s = 1
    while s < x.shape[0]:
        x = x * jnp.where(row >= s, pltpu.roll(x, s, 0), 1.0)
        s *= 2
    return x


def _block_kernel(x_ref, pre_w_ref, w_in_ref, conv_w_ref, conv_b_ref, wa_ref, ba_ref, wx_ref, bx_ref,
                  loga_ref, lb_ref, gw_ref, w_out_ref, post_w_ref, out_ref,
                  u_ref, xpad_ref, p_ref, xcb_ref, ra_ref, ri_ref, ycat_ref, hc_ref, st_ref):
    t = pl.program_id(1)

    @pl.when(t == 0)
    def _():
        xpad_ref[0:SUBLANES, :] = jnp.zeros((SUBLANES, D_LRU), f32)
        hc_ref[...] = jnp.zeros_like(hc_ref)
        st_ref[...] = jnp.zeros_like(st_ref)

    pre_w = pre_w_ref[...]

    def norm_body(i, c):
        r0 = pl.multiple_of(i * NORM_ROWS, NORM_ROWS)
        xs = x_ref[pl.ds(r0, NORM_ROWS), :]
        ms = jnp.mean(xs * xs, axis=-1, keepdims=True)
        u_ref[pl.ds(r0, NORM_ROWS), :] = (xs * lax.rsqrt(ms + EPS) * pre_w).astype(bf16)
        return c

    lax.fori_loop(0, TILE // NORM_ROWS, norm_body, 0)

    u = u_ref[...]
    xpad_ref[SUBLANES:SUBLANES + TILE, :] = jnp.dot(u, w_in_ref[0], preferred_element_type=f32)
    for sec in range(5):
        p_ref[sec] = jnp.dot(u, w_in_ref[sec + 1], preferred_element_type=f32)

    conv_b = conv_b_ref[...]
    for j in range(TILE // CHUNK):
        base = j * CHUNK + SUBLANES
        xc = conv_b
        for k in range(CONV_W):
            off = base - (CONV_W - 1) + k
            xc = xc + conv_w_ref[k:k + 1, :] * xpad_ref[off:off + CHUNK, :]
        xcb_ref[j * CHUNK:(j + 1) * CHUNK, :] = xc.astype(bf16)
        xpad_ref[base - SUBLANES:base - SUBLANES + CHUNK, :] = xc

    for n in range(LRU_BLOCKS):
        cols = slice(n * LRU_BLOCK_W, (n + 1) * LRU_BLOCK_W)
        xb = xcb_ref[:, cols]
        ra_ref[:, cols] = jnp.dot(xb, wa_ref[n], preferred_element_type=f32)
        ri_ref[:, cols] = jnp.dot(xb, wx_ref[n], preferred_element_type=f32)

    ba = ba_ref[...]
    bx = bx_ref[...]
    loga_c = loga_ref[...]

    def lru_body(i, carry):
        r0 = pl.multiple_of(i * SLAB, SLAB)
        hs = []
        for half in range(SLAB // SUBLANES):
            rows = pl.ds(r0 + half * SUBLANES, SUBLANES)
            xc = xpad_ref[rows, :]
            r = _sigmoid(ra_ref[rows, :] + ba)
            ig = _sigmoid(ri_ref[rows, :] + bx)
            log_a = loga_c * r
            a = jnp.exp(log_a)
            gated = jnp.sqrt(1.0 - a * a) * (ig * xc)
            a_cum, h_loc = _scan8(a, gated)
            h = h_loc + a_cum * carry
            carry = jnp.broadcast_to(h[SUBLANES - 1:SUBLANES, :], h.shape)
            hs.append(h * _silu(p_ref[0, rows, :]))
        ycat_ref[pl.ds(r0, SLAB), 0:D_LRU] = jnp.concatenate(hs, axis=0).astype(bf16)
        return carry

    hc_ref[...] = lax.fori_loop(0, TILE // SLAB, lru_body, hc_ref[...])
    xpad_ref[0:SUBLANES, :] = xpad_ref[TILE:TILE + SUBLANES, :]

    lb = lb_ref[...]
    gw = gw_ref[...]
    tri = (lax.broadcasted_iota(jnp.int32, (CHUNK, CHUNK), 0)
           >= lax.broadcasted_iota(jnp.int32, (CHUNK, CHUNK), 1))

    def hgrn_body(c, carry):
        r0 = pl.multiple_of(c * CHUNK, CHUNK)
        rows = pl.ds(r0, CHUNK)
        for hd in range(HEADS):
            cols = slice(hd * HEAD_W, (hd + 1) * HEAD_W)
            lbh = lb[:, cols]
            f = lbh + (1.0 - lbh) * _sigmoid(p_ref[2, rows, cols])
            decay = _cumprod_rows(f)
            inv = 1.0 / decay
            d_last = decay[CHUNK - 1:CHUNK, :]
            kk = 1.0 - f
            qd = (_silu(p_ref[1, rows, cols]) * decay).astype(bf16)
            kd = (kk * inv).astype(bf16)
            ke = (kk * (inv * d_last)).astype(bf16)
            v = p_ref[3, rows, cols].astype(bf16)
            st = st_ref[hd]
            scores = lax.dot_general(qd, kd, (((1,), (1,)), ((), ())), preferred_element_type=f32)
            scores = jnp.where(tri, scores, 0.0).astype(bf16)
            o = jnp.dot(scores, v, preferred_element_type=f32)
            o = o + lax.dot_general(qd, st.astype(bf16), (((1,), (1,)), ((), ())), preferred_element_type=f32)
            st_ref[hd] = st * d_last + lax.dot_general(v, ke, (((0,), (0,)), ((), ())),
                                                       preferred_element_type=f32)
            ms = jnp.mean(o * o, axis=-1, keepdims=True)
            o = o * lax.rsqrt(ms + EPS) * gw[:, cols]
            y = o * _silu(p_ref[4, rows, cols])
            ycat_ref[rows, D_LRU + hd * HEAD_W:D_LRU + (hd + 1) * HEAD_W] = y.astype(bf16)
        return carry

    lax.fori_loop(0, TILE // CHUNK, hgrn_body, 0)

    y = jnp.dot(ycat_ref[...], w_out_ref[...], preferred_element_type=f32)
    ms = jnp.mean(y * y, axis=-1, keepdims=True)
    out_ref[...] = x_ref[...] + y * lax.rsqrt(ms + EPS) * post_w_ref[...]


def _resident(shape):
    zeros = (0,) * len(shape)
    return pl.BlockSpec(shape, lambda b, t: zeros, pipeline_mode=pl.Buffered(1))


@jax.jit
def kernel(x, pre_norm_w, w_in, conv_w, conv_b, lru_w_a, lru_b_a, lru_w_x, lru_b_x, lru_lambda,
           hgrn_lb_logits, hgrn_gnorm_w, w_out, post_norm_w):
    bsz, seq, _ = x.shape
    assert seq % TILE == 0 and pre_norm_w.shape[0] == 1
    w_in_g = w_in[0].reshape(D_MODEL, 6, D_LRU).transpose(1, 0, 2).astype(bf16)
    lower = jnp.cumsum(jax.nn.softmax(hgrn_lb_logits.astype(f32), axis=0), axis=0)[0:1]
    loga_c = (-LRU_C * jax.nn.softplus(-lru_lambda[0].astype(f32))).reshape(1, D_LRU)
    row = lambda a: a.reshape(1, -1).astype(f32)

    tile_spec = pl.BlockSpec((None, TILE, D_MODEL), lambda b, t: (b, t, 0))
    return pl.pallas_call(
        _block_kernel,
        out_shape=jax.ShapeDtypeStruct(x.shape, x.dtype),
        grid=(bsz, seq // TILE),
        in_specs=[
            tile_spec,
            _resident((1, D_MODEL)),
            _resident((6, D_MODEL, D_LRU)),
            _resident((CONV_W, D_LRU)),
            _resident((1, D_LRU)),
            _resident((LRU_BLOCKS, LRU_BLOCK_W, LRU_BLOCK_W)),
            _resident((1, D_LRU)),
            _resident((LRU_BLOCKS, LRU_BLOCK_W, LRU_BLOCK_W)),
            _resident((1, D_LRU)),
            _resident((1, D_LRU)),
            _resident((1, D_HGRN)),
            _resident((1, D_HGRN)),
            _resident((D_LRU + D_HGRN, D_MODEL)),
            _resident((1, D_MODEL)),
        ],
        out_specs=tile_spec,
        scratch_shapes=[
            pltpu.VMEM((TILE, D_MODEL), bf16),
            pltpu.VMEM((TILE + SUBLANES, D_LRU), f32),
            pltpu.VMEM((5, TILE, D_LRU), f32),
            pltpu.VMEM((TILE, D_LRU), bf16),
            pltpu.VMEM((TILE, D_LRU), f32),
            pltpu.VMEM((TILE, D_LRU), f32),
            pltpu.VMEM((TILE, D_LRU + D_HGRN), bf16),
            pltpu.VMEM((SUBLANES, D_LRU), f32),
            pltpu.VMEM((HEADS, HEAD_W, HEAD_W), f32),
        ],
        compiler_params=pltpu.CompilerParams(
            dimension_semantics=("arbitrary", "arbitrary"),
            vmem_limit_bytes=VMEM_LIMIT_BYTES,
        ),
        name="hymba_block",
    )(x, row(pre_norm_w[0]), w_in_g, conv_w[0].astype(f32), row(conv_b[0]),
      lru_w_a[0].astype(bf16), row(lru_b_a[0]), lru_w_x[0].astype(bf16), row(lru_b_x[0]),
      loga_c, lower, row(hgrn_gnorm_w[0]), w_out[0].astype(bf16), row(post_norm_w[0]))
```
